```python
import jax, jax.numpy as jnp
from jax import lax
import numpy as np

D_MODEL = 1024
BATCH = 32
SEQ = 2048
DEPTH = 2

CHUNK = 64
EPS = 1e-6
NEG_INF = -1e30

CONV_WIDTH = D_MODEL // 2
CONV_KERNEL = 31
HEAD_DIM = 64
ATTN_HEADS = D_MODEL // 128
ATTN_WIDTH = ATTN_HEADS * HEAD_DIM
LEFT_CHUNKS = 8
BAND = (LEFT_CHUNKS + 1) * CHUNK
KEY_PAD = BAND - CHUNK
MAX_REL = 256
POOL_WINDOWS = (2, 4, 8, 16)
POOL_GROUPS = len(POOL_WINDOWS)
POOL_WIDTH = D_MODEL // 2
POOL_GROUP_DIM = POOL_WIDTH // POOL_GROUPS
N_BRANCH = 3

IN_SPLITS = (CONV_WIDTH, CONV_WIDTH, CONV_WIDTH,
             ATTN_WIDTH, ATTN_WIDTH, ATTN_WIDTH, ATTN_WIDTH,
             POOL_WIDTH, POOL_WIDTH,
             N_BRANCH * D_MODEL)
IN_COLS = sum(IN_SPLITS)

kernel_name = "hybrid_conv_chunkattn_pool_gated_block"


def rms_norm(x, g):
    xf = x.astype(jnp.float32)
    y = xf * lax.rsqrt(jnp.mean(xf * xf, axis=-1, keepdims=True) + EPS)
    return (y * g.astype(jnp.float32)).astype(x.dtype)


def layer_norm(x, g, b):
    xf = x.astype(jnp.float32)
    mu = jnp.mean(xf, axis=-1, keepdims=True)
    xc = xf - mu
    y = xc * lax.rsqrt(jnp.mean(xc * xc, axis=-1, keepdims=True) + EPS)
    return (y * g.astype(jnp.float32) + b.astype(jnp.float32)).astype(x.dtype)


def conv_branch(a, b, gate, dw, dw_b, ln_g, ln_b, w_o):
    u = a * jax.nn.sigmoid(b)
    u = lax.conv_general_dilated(
        u, dw[:, None, :], window_strides=(1,),
        padding=[(CONV_KERNEL - 1, 0)],
        dimension_numbers=('NWC', 'WIO', 'NWC'),
        feature_group_count=CONV_WIDTH) + dw_b
    u = jax.nn.silu(layer_norm(u, ln_g, ln_b))
    return (u * jax.nn.silu(gate)) @ w_o


def chunk_attention_branch(q, k, v, gate, rel_table, w_o):
    B, S, _ = q.shape
    n_chunks = S // CHUNK
    qc = (q * (HEAD_DIM ** -0.5)).reshape(B, n_chunks, CHUNK, ATTN_HEADS, HEAD_DIM)
    qc = qc.transpose(1, 0, 3, 2, 4)
    kh = k.reshape(B, S, ATTN_HEADS, HEAD_DIM).transpose(0, 2, 1, 3)
    vh = v.reshape(B, S, ATTN_HEADS, HEAD_DIM).transpose(0, 2, 1, 3)
    kh = jnp.pad(kh, ((0, 0), (0, 0), (KEY_PAD, 0), (0, 0)))
    vh = jnp.pad(vh, ((0, 0), (0, 0), (KEY_PAD, 0), (0, 0)))
    rel = jnp.arange(CHUNK)[:, None] + KEY_PAD - jnp.arange(BAND)[None, :]
    bias = rel_table[:, jnp.clip(rel, -MAX_REL, MAX_REL) + MAX_REL].astype(jnp.float32)
    key_offsets = jnp.arange(BAND) - KEY_PAD

    def one_chunk(args):
        q_blk, c = args
        start = c * CHUNK
        kb = lax.dynamic_slice_in_dim(kh, start, BAND, axis=2)
        vb = lax.dynamic_slice_in_dim(vh, start, BAND, axis=2)
        s = jnp.einsum('bhqd,bhkd->bhqk', q_blk, kb).astype(jnp.float32) + bias
        valid = (start + key_offsets) >= 0
        s = jnp.where(valid, s, NEG_INF)
        p = jax.nn.softmax(s, axis=-1).astype(vb.dtype)
        return jnp.einsum('bhqk,bhkd->bhqd', p, vb)

    o = lax.map(one_chunk, (qc, jnp.arange(n_chunks)))
    o = o.transpose(1, 0, 3, 2, 4).reshape(B, S, ATTN_WIDTH)
    return (o * jax.nn.silu(gate)) @ w_o


def pool_branch(u, gate, w_grp, b_grp, scale, w_o):
    B, S, _ = u.shape
    uf = u.astype(jnp.float32)
    cs = jnp.pad(jnp.cumsum(uf, axis=1), ((0, 0), (1, 0), (0, 0)))
    t = jnp.arange(S)
    outs = []
    for g, w in enumerate(POOL_WINDOWS):
        sl = slice(g * POOL_GROUP_DIM, (g + 1) * POOL_GROUP_DIM)
        csg = cs[..., sl]
        lower = jnp.concatenate(
            [jnp.zeros((B, w - 1, POOL_GROUP_DIM), jnp.float32), csg[:, :S + 1 - w]], axis=1)
        cnt = jnp.minimum(t + 1, w).astype(jnp.float32)[None, :, None]
        outs.append((csg[:, 1:] - lower) / cnt - uf[..., sl])
    pooled = jnp.stack(outs, axis=2).astype(u.dtype)
    mixed = jnp.einsum('bsgc,gcd->bsgd', pooled, w_grp) + b_grp
    mixed = mixed.reshape(B, S, POOL_WIDTH) * scale
    return (mixed * jax.nn.silu(gate)) @ w_o


def hybrid_layer(x, pre_g, post_g, w_in, conv_dw, conv_dw_b, conv_ln_g, conv_ln_b,
                 w_conv_out, rel_bias, w_attn_out, pool_w, pool_b, pool_scale,
                 w_pool_out, w_out):
    B, S, D = x.shape
    h = rms_norm(x, pre_g)
    z = h @ w_in
    (c_a, c_b, c_gate, q, k, v, a_gate, p_in, p_gate, g_merge) = jnp.split(
        z, np.cumsum(IN_SPLITS)[:-1].tolist(), axis=-1)
    y_conv = conv_branch(c_a, c_b, c_gate, conv_dw, conv_dw_b, conv_ln_g, conv_ln_b, w_conv_out)
    y_attn = chunk_attention_branch(q, k, v, a_gate, rel_bias, w_attn_out)
    y_pool = pool_branch(p_in, p_gate, pool_w, pool_b, pool_scale, w_pool_out)
    gates = jax.nn.sigmoid(g_merge).reshape(B, S, N_BRANCH, D)
    merged = gates[:, :, 0] * y_conv + gates[:, :, 1] * y_attn + gates[:, :, 2] * y_pool
    y = merged @ w_out
    return x + rms_norm(y, post_g)


def setup_inputs(seed: int = 0) -> dict:
    key = jax.random.key(seed)
    ks = jax.random.split(key, 20)
    n = lambda k, shape, s: jax.random.normal(k, shape, jnp.float32) * s
    L, D = DEPTH, D_MODEL
    return {
        "x": n(ks[0], (BATCH, SEQ, D), 1.0),
        "pre_norm_g": 1.0 + n(ks[1], (L, D), 0.05),
        "post_norm_g": 1.0 + n(ks[2], (L, D), 0.05),
        "w_in": n(ks[3], (L, D, IN_COLS), D ** -0.5),
        "conv_dw": n(ks[4], (L, CONV_KERNEL, CONV_WIDTH), CONV_KERNEL ** -0.5),
        "conv_dw_b": n(ks[5], (L, CONV_WIDTH), 0.02),
        "conv_ln_g": 1.0 + n(ks[6], (L, CONV_WIDTH), 0.05),
        "conv_ln_b": n(ks[7], (L, CONV_WIDTH), 0.02),
        "w_conv_out": n(ks[8], (L, CONV_WIDTH, D), CONV_WIDTH ** -0.5),
        "rel_bias": n(ks[9], (L, ATTN_HEADS, 2 * MAX_REL + 1), 0.1),
        "w_attn_out": n(ks[10], (L, ATTN_WIDTH, D), ATTN_WIDTH ** -0.5),
        "pool_w": n(ks[11], (L, POOL_GROUPS, POOL_GROUP_DIM, POOL_GROUP_DIM), POOL_GROUP_DIM ** -0.5),
        "pool_b": n(ks[12], (L, POOL_GROUPS, POOL_GROUP_DIM), 0.02),
        "pool_scale": 1.0 + n(ks[13], (L, POOL_WIDTH), 0.1),
        "w_pool_out": n(ks[14], (L, POOL_WIDTH, D), POOL_WIDTH ** -0.5),
        "w_out": n(ks[15], (L, D, D), D ** -0.5),
    }


def reference(x, pre_norm_g, post_norm_g, w_in, conv_dw, conv_dw_b, conv_ln_g, conv_ln_b,
              w_conv_out, rel_bias, w_attn_out, pool_w, pool_b, pool_scale, w_pool_out, w_out):
    for l in range(DEPTH):
        x = hybrid_layer(x, pre_norm_g[l], post_norm_g[l], w_in[l], conv_dw[l], conv_dw_b[l],
                         conv_ln_g[l], conv_ln_b[l], w_conv_out[l], rel_bias[l], w_attn_out[l],
                         pool_w[l], pool_b[l], pool_scale[l], w_pool_out[l], w_out[l])
    return x
```

```python
import functools

import jax
import jax.numpy as jnp
import numpy as np
from jax import lax
from jax.experimental import pallas as pl
from jax.experimental.pallas import tpu as pltpu

F32 = jnp.float32
BF16 = jnp.bfloat16

EPS = 1e-6
NEG_INF = -1e30

CHUNK = 64
HEAD_DIM = 64
LEFT_CHUNKS = 8
BAND = (LEFT_CHUNKS + 1) * CHUNK
KEY_PAD = BAND - CHUNK
MAX_REL = 256
CONV_KERNEL = 31
POOL_WINDOWS = (2, 4, 8, 16)
POOL_GROUP_DIM = 128
N_BRANCH = 3

LANES = 128
SEQ_TILE = KEY_PAD
CONV_HALO = 32
POOL_HALO = 16
ROW_BLOCK = CHUNK
OUT_ROWS = 256
INPROJ_TM = 1024
INPROJ_TN = 1536
VMEM_LIMIT_BYTES = 56 * 1024 * 1024


def _sigmoid(v):
    return 1.0 / (1.0 + jnp.exp(-v))


def _silu(v):
    return v * _sigmoid(v)


def _inproj_kernel(x_ref, g_ref, w_ref, z_ref, h_ref):
    @pl.when(pl.program_id(1) == 0)
    def _():
        x = x_ref[...]
        ms = jnp.mean(x * x, axis=-1, keepdims=True)
        h_ref[...] = (x * lax.rsqrt(ms + EPS) * g_ref[...]).astype(BF16)

    z_ref[...] = jnp.dot(h_ref[...], w_ref[...], preferred_element_type=F32).astype(BF16)


def _inproj(x2, g, w):
    tokens, d = x2.shape
    ncol = w.shape[1]
    return pl.pallas_call(
        _inproj_kernel,
        grid=(tokens // INPROJ_TM, ncol // INPROJ_TN),
        in_specs=[
            pl.BlockSpec((INPROJ_TM, d), lambda i, j: (i, 0)),
            pl.BlockSpec((1, d), lambda i, j: (0, 0)),
            pl.BlockSpec((d, INPROJ_TN), lambda i, j: (0, j)),
        ],
        out_specs=pl.BlockSpec((INPROJ_TM, INPROJ_TN), lambda i, j: (i, j)),
        out_shape=jax.ShapeDtypeStruct((tokens, ncol), BF16),
        scratch_shapes=[pltpu.VMEM((INPROJ_TM, d), BF16)],
        compiler_params=pltpu.CompilerParams(
            dimension_semantics=("arbitrary", "arbitrary"),
            vmem_limit_bytes=VMEM_LIMIT_BYTES),
        name="inproj",
    )(x2, g, w)


def _mixer_kernel(x_ref, za_ref, zp_ref, zg_ref, zc_ref,
                  dw_ref, dwb_ref, lng_ref, lnb_ref, wco_ref,
                  bias_ref, wao_ref,
                  pw_ref, pb_ref, ps_ref, wpo_ref,
                  wout_ref, postg_ref,
                  o_ref,
                  ubuf, pbuf, kbuf, vbuf, pooled, act_c, act_a, act_p):
    ts = SEQ_TILE
    wc = ubuf.shape[1]
    t = pl.program_id(1)

    @pl.when(t == 0)
    def _():
        ubuf[0:CONV_HALO, :] = jnp.zeros((CONV_HALO, wc), F32)
        pbuf[0:POOL_HALO, :] = jnp.zeros((POOL_HALO, wc), F32)
        kbuf[0:ts, :] = jnp.zeros((ts, wc), BF16)
        vbuf[0:ts, :] = jnp.zeros((ts, wc), BF16)

    ca = zc_ref[:, 0:wc].astype(F32)
    cb = zc_ref[:, wc:2 * wc].astype(F32)
    ubuf[CONV_HALO:CONV_HALO + ts, :] = ca * _sigmoid(cb)
    pbuf[POOL_HALO:POOL_HALO + ts, :] = zp_ref[:, 0:wc].astype(F32)
    kbuf[ts:2 * ts, :] = za_ref[:, wc:2 * wc]
    vbuf[ts:2 * ts, :] = za_ref[:, 2 * wc:3 * wc]

    def conv_rows(rb):
        base = rb * ROW_BLOCK
        acc = jnp.zeros((ROW_BLOCK, wc), F32) + dwb_ref[...]
        for k in range(CONV_KERNEL):
            off = CONV_HALO - (CONV_KERNEL - 1) + k
            acc = acc + ubuf[pl.ds(base + off, ROW_BLOCK), :] * dw_ref[k:k + 1, :]
        mu = jnp.mean(acc, axis=-1, keepdims=True)
        xc = acc - mu
        var = jnp.mean(xc * xc, axis=-1, keepdims=True)
        yn = xc * lax.rsqrt(var + EPS) * lng_ref[...] + lnb_ref[...]
        gate = zc_ref[pl.ds(base, ROW_BLOCK), 2 * wc:3 * wc].astype(F32)
        act_c[pl.ds(base, ROW_BLOCK), :] = (_silu(yn) * _silu(gate)).astype(BF16)

    for rb in range(ts // ROW_BLOCK):
        conv_rows(rb)

    def pool_rows(rb):
        base = rb * ROW_BLOCK
        pos = (t * ts + base + 1
               + lax.broadcasted_iota(jnp.int32, (ROW_BLOCK, POOL_GROUP_DIM), 0))
        for g, w in enumerate(POOL_WINDOWS):
            sl = slice(g * POOL_GROUP_DIM, (g + 1) * POOL_GROUP_DIM)
            tok = pbuf[pl.ds(base + POOL_HALO, ROW_BLOCK), sl]
            sm = tok
            for j in range(1, w):
                sm = sm + pbuf[pl.ds(base + POOL_HALO - j, ROW_BLOCK), sl]
            cnt = jnp.minimum(pos, w).astype(F32)
            pooled[pl.ds(base, ROW_BLOCK), sl] = (sm / cnt - tok).astype(BF16)

    for rb in range(ts // ROW_BLOCK):
        pool_rows(rb)

    for g in range(len(POOL_WINDOWS)):
        sl = slice(g * POOL_GROUP_DIM, (g + 1) * POOL_GROUP_DIM)
        mixed = jnp.dot(pooled[:, sl], pw_ref[g], preferred_element_type=F32) + pb_ref[g:g + 1, :]
        gate = zp_ref[:, wc + g * POOL_GROUP_DIM:wc + (g + 1) * POOL_GROUP_DIM].astype(F32)
        act_p[:, sl] = (mixed * ps_ref[:, sl] * _silu(gate)).astype(BF16)

    n_pairs = wc // LANES
    lane = lax.broadcasted_iota(jnp.int32, (CHUNK, LANES), 1)
    col = lax.broadcasted_iota(jnp.int32, (CHUNK, BAND), 1)
    first_valid = jnp.where(t == 0, KEY_PAD, 0)

    def attn_chunk(c, carry):
        qbase = pl.multiple_of(c * CHUNK, CHUNK)
        valid = col >= first_valid - c * CHUNK
        for hp in range(n_pairs):
            ls = slice(hp * LANES, (hp + 1) * LANES)
            q = za_ref[pl.ds(qbase, CHUNK), ls] * jnp.asarray(HEAD_DIM ** -0.5, BF16)
            kb = kbuf[pl.ds(qbase, BAND), ls]
            vb = vbuf[pl.ds(qbase, BAND), ls]
            outs = []
            for hh in range(2):
                in_head = (lane >= hh * HEAD_DIM) & (lane < (hh + 1) * HEAD_DIM)
                qm = jnp.where(in_head, q, jnp.zeros_like(q))
                s = lax.dot_general(qm, kb, (((1,), (1,)), ((), ())),
                                    preferred_element_type=F32)
                s = s + bias_ref[hp * 2 + hh]
                s = jnp.where(valid, s, NEG_INF)
                m = jnp.max(s, axis=-1, keepdims=True)
                p = jnp.exp(s - m)
                l = jnp.sum(p, axis=-1, keepdims=True)
                o = jnp.dot(p.astype(BF16), vb, preferred_element_type=F32)
                outs.append(o / l)
            o_pair = jnp.where(lane < HEAD_DIM, outs[0], outs[1])
            gate = za_ref[pl.ds(qbase, CHUNK), 3 * wc + hp * LANES:3 * wc + (hp + 1) * LANES]
            act_a[pl.ds(qbase, CHUNK), ls] = (o_pair * _silu(gate.astype(F32))).astype(BF16)
        return carry

    lax.fori_loop(0, ts // CHUNK, attn_chunk, 0)

    d = o_ref.shape[1]

    def out_rows(rb, carry):
        base = pl.multiple_of(rb * OUT_ROWS, OUT_ROWS)
        rows = pl.ds(base, OUT_ROWS)
        y_c = jnp.dot(act_c[rows, :], wco_ref[...], preferred_element_type=F32)
        y_a = jnp.dot(act_a[rows, :], wao_ref[...], preferred_element_type=F32)
        y_p = jnp.dot(act_p[rows, :], wpo_ref[...], preferred_element_type=F32)
        merged = (_sigmoid(zg_ref[rows, 0:d].astype(F32)) * y_c
                  + _sigmoid(zg_ref[rows, d:2 * d].astype(F32)) * y_a
                  + _sigmoid(zg_ref[rows, 2 * d:3 * d].astype(F32)) * y_p)
        y = jnp.dot(merged.astype(BF16), wout_ref[...], preferred_element_type=F32)
        ms = jnp.mean(y * y, axis=-1, keepdims=True)
        o_ref[rows, :] = x_ref[rows, :] + y * lax.rsqrt(ms + EPS) * postg_ref[...]
        return carry

    lax.fori_loop(0, ts // OUT_ROWS, out_rows, 0)

    ubuf[0:CONV_HALO, :] = ubuf[ts:ts + CONV_HALO, :]
    pbuf[0:POOL_HALO, :] = pbuf[ts:ts + POOL_HALO, :]
    kbuf[0:ts, :] = kbuf[ts:2 * ts, :]
    vbuf[0:ts, :] = vbuf[ts:2 * ts, :]


def _mixer(x2, z, p, batch, seq):
    tokens, d = x2.shape
    ts = SEQ_TILE
    nt = seq // ts
    wc = p["dw"].shape[1]
    row = lambda b, t: b * nt + t
    full = lambda shape: pl.BlockSpec(shape, lambda b, t: (0,) * len(shape))
    in_specs = [
        pl.BlockSpec((ts, d), lambda b, t: (row(b, t), 0)),
        pl.BlockSpec((ts, 4 * wc), lambda b, t: (row(b, t), 0)),
        pl.BlockSpec((ts, 2 * wc), lambda b, t: (row(b, t), 2)),
        pl.BlockSpec((ts, N_BRANCH * d), lambda b, t: (row(b, t), 1)),
        pl.BlockSpec((ts, 3 * wc), lambda b, t: (row(b, t), 4)),
        full(p["dw"].shape), full((1, wc)), full((1, wc)), full((1, wc)), full((wc, d)),
        full(p["bias"].shape), full((wc, d)),
        full(p["pw"].shape), full(p["pb"].shape), full((1, wc)), full((wc, d)),
        full((d, d)), full((1, d)),
    ]
    return pl.pallas_call(
        _mixer_kernel,
        grid=(batch, nt),
        in_specs=in_specs,
        out_specs=pl.BlockSpec((ts, d), lambda b, t: (row(b, t), 0)),
        out_shape=jax.ShapeDtypeStruct((tokens, d), F32),
        scratch_shapes=[
            pltpu.VMEM((CONV_HALO + ts, wc), F32),
            pltpu.VMEM((POOL_HALO + ts, wc), F32),
            pltpu.VMEM((2 * ts, wc), BF16),
            pltpu.VMEM((2 * ts, wc), BF16),
            pltpu.VMEM((ts, wc), BF16),
            pltpu.VMEM((ts, wc), BF16),
            pltpu.VMEM((ts, wc), BF16),
            pltpu.VMEM((ts, wc), BF16),
        ],
        compiler_params=pltpu.CompilerParams(
            dimension_semantics=("arbitrary", "arbitrary"),
            vmem_limit_bytes=VMEM_LIMIT_BYTES),
        name="mixer",
    )(x2, z, z, z, z,
      p["dw"], p["dwb"], p["lng"], p["lnb"], p["wco"],
      p["bias"], p["wao"],
      p["pw"], p["pb"], p["ps"], p["wpo"],
      p["wout"], p["postg"])


def _rel_index():
    rel = np.arange(CHUNK)[:, None] + KEY_PAD - np.arange(BAND)[None, :]
    return np.clip(rel, -MAX_REL, MAX_REL) + MAX_REL


def _permute_in_cols(w_in, wc, d):
    conv = w_in[:, 0:3 * wc]
    attn = w_in[:, 3 * wc:7 * wc]
    pool = w_in[:, 7 * wc:9 * wc]
    merge = w_in[:, 9 * wc:9 * wc + N_BRANCH * d]
    return jnp.concatenate([attn, pool, merge, conv], axis=1)


def kernel(x, pre_norm_g, post_norm_g, w_in, conv_dw, conv_dw_b, conv_ln_g, conv_ln_b,
           w_conv_out, rel_bias, w_attn_out, pool_w, pool_b, pool_scale, w_pool_out, w_out):
    batch, seq, d = x.shape
    depth = w_in.shape[0]
    wc = conv_dw.shape[2]
    assert seq % SEQ_TILE == 0 and (batch * seq) % INPROJ_TM == 0
    assert w_in.shape[2] % INPROJ_TN == 0
    rel_idx = _rel_index()
    x2 = x.reshape(batch * seq, d)
    for l in range(depth):
        w_l = _permute_in_cols(w_in[l], wc, d).astype(BF16)
        params = dict(
            dw=jnp.pad(conv_dw[l], ((0, 1), (0, 0))),
            dwb=conv_dw_b[l].reshape(1, wc),
            lng=conv_ln_g[l].reshape(1, wc),
            lnb=conv_ln_b[l].reshape(1, wc),
            wco=w_conv_out[l].astype(BF16),
            bias=rel_bias[l][:, rel_idx].astype(F32),
            wao=w_attn_out[l].astype(BF16),
            pw=pool_w[l].astype(BF16),
            pb=pool_b[l],
            ps=pool_scale[l].reshape(1, wc),
            wpo=w_pool_out[l].astype(BF16),
            wout=w_out[l].astype(BF16),
            postg=post_norm_g[l].reshape(1, d),
        )
        z = _inproj(x2, pre_norm_g[l].reshape(1, d), w_l)
        x2 = _mixer(x2, z, params, batch, seq)
    return x2.reshape(batch, seq, d)
```

```python
import jax
import jax.numpy as jnp
import numpy as np
from jax import lax
from jax.experimental import pallas as pl
from jax.experimental.pallas import tpu as pltpu

F32 = jnp.float32
BF16 = jnp.bfloat16

EPS = 1e-6
NEG_INF = -1e30

CHUNK = 64
HEAD_DIM = 64
LEFT_CHUNKS = 8
BAND = (LEFT_CHUNKS + 1) * CHUNK
KEY_PAD = BAND - CHUNK
MAX_REL = 256
CONV_KERNEL = 31
POOL_WINDOWS = (2, 4, 8, 16)
POOL_GROUP_DIM = 128
N_BRANCH = 3

LANES = 128
SUBLANES = 8
SEQ_TILE = KEY_PAD
HALF = SEQ_TILE // 2
HALF_KEYS = HALF + KEY_PAD
HEADS_PER_PAIR = LANES // HEAD_DIM
CONV_HALO = 32
POOL_HALO = 16
ROW_BLOCK = CHUNK
OUT_ROWS = 256
INPROJ_TM = 1024
INPROJ_TN = 1536
VMEM_LIMIT_BYTES = 56 * 1024 * 1024


def _sigmoid(v):
    return 1.0 / (1.0 + jnp.exp(-v))


def _silu(v):
    return v * _sigmoid(v)


def _shift_up(win, r):
    if r == 0:
        return win
    return pltpu.roll(win, win.shape[0] - r, axis=0)


def _inproj_kernel(x_ref, g_ref, w_ref, z_ref, h_ref):
    @pl.when(pl.program_id(1) == 0)
    def _():
        x = x_ref[...]
        ms = jnp.mean(x * x, axis=-1, keepdims=True)
        h_ref[...] = (x * lax.rsqrt(ms + EPS) * g_ref[...]).astype(BF16)

    z_ref[...] = jnp.dot(h_ref[...], w_ref[...], preferred_element_type=F32).astype(BF16)


def _inproj(x2, g, w):
    tokens, d = x2.shape
    ncol = w.shape[1]
    return pl.pallas_call(
        _inproj_kernel,
        grid=(tokens // INPROJ_TM, ncol // INPROJ_TN),
        in_specs=[
            pl.BlockSpec((INPROJ_TM, d), lambda i, j: (i, 0)),
            pl.BlockSpec((1, d), lambda i, j: (0, 0)),
            pl.BlockSpec((d, INPROJ_TN), lambda i, j: (0, j)),
        ],
        out_specs=pl.BlockSpec((INPROJ_TM, INPROJ_TN), lambda i, j: (i, j)),
        out_shape=jax.ShapeDtypeStruct((tokens, ncol), BF16),
        scratch_shapes=[pltpu.VMEM((INPROJ_TM, d), BF16)],
        compiler_params=pltpu.CompilerParams(
            dimension_semantics=("arbitrary", "arbitrary"),
            vmem_limit_bytes=VMEM_LIMIT_BYTES),
        name="inproj",
    )(x2, g, w)


def _mixer_kernel(x_ref, za_ref, zp_ref, zg_ref, zc_ref,
                  dw_ref, dwb_ref, lng_ref, lnb_ref, wco_ref,
                  bias_ref, wao_ref,
                  pw_ref, pb_ref, ps_ref, wpo_ref,
                  wout_ref, postg_ref,
                  o_ref,
                  ubuf, pbuf, qbuf, kbuf, vbuf, obuf, pooled, act_c, act_a, act_p,
                  s_even, s_odd, p_even, p_odd, l_even, l_odd):
    sbufs, pbufs, lbufs = (s_even, s_odd), (p_even, p_odd), (l_even, l_odd)
    ts = SEQ_TILE
    wc = ubuf.shape[1]
    n_pairs = wc // LANES
    t = pl.program_id(1)

    @pl.when(t == 0)
    def _():
        ubuf[0:CONV_HALO, :] = jnp.zeros((CONV_HALO, wc), F32)
        pbuf[0:POOL_HALO, :] = jnp.zeros((POOL_HALO, wc), F32)
        kbuf[:, 0:ts, :] = jnp.zeros((n_pairs, ts, LANES), BF16)
        vbuf[:, 0:ts, :] = jnp.zeros((n_pairs, ts, LANES), BF16)

    ca = zc_ref[:, 0:wc].astype(F32)
    cb = zc_ref[:, wc:2 * wc].astype(F32)
    ubuf[CONV_HALO:CONV_HALO + ts, :] = ca * _sigmoid(cb)
    pbuf[POOL_HALO:POOL_HALO + ts, :] = zp_ref[:, 0:wc].astype(F32)
    for hp in range(n_pairs):
        ls = slice(hp * LANES, (hp + 1) * LANES)
        qbuf[hp] = za_ref[:, ls] * jnp.asarray(HEAD_DIM ** -0.5, BF16)
        kbuf[hp, ts:2 * ts, :] = za_ref[:, wc + hp * LANES:wc + (hp + 1) * LANES]
        vbuf[hp, ts:2 * ts, :] = za_ref[:, 2 * wc + hp * LANES:2 * wc + (hp + 1) * LANES]

    conv_win = ROW_BLOCK + CONV_HALO
    first_off = CONV_HALO - (CONV_KERNEL - 1)

    def conv_rows(rb, carry):
        base = pl.multiple_of(rb * ROW_BLOCK, ROW_BLOCK)
        win = ubuf[pl.ds(base, conv_win), :]
        acc = jnp.zeros((ROW_BLOCK, wc), F32) + dwb_ref[...]
        for r in range(SUBLANES):
            wr = _shift_up(win, r)
            for a in range(conv_win // SUBLANES):
                k = a * SUBLANES + r - first_off
                if 0 <= k < CONV_KERNEL:
                    acc = acc + wr[a * SUBLANES:a * SUBLANES + ROW_BLOCK, :] * dw_ref[k:k + 1, :]
        mu = jnp.mean(acc, axis=-1, keepdims=True)
        xc = acc - mu
        var = jnp.mean(xc * xc, axis=-1, keepdims=True)
        yn = xc * lax.rsqrt(var + EPS) * lng_ref[...] + lnb_ref[...]
        gate = zc_ref[pl.ds(base, ROW_BLOCK), 2 * wc:3 * wc].astype(F32)
        act_c[pl.ds(base, ROW_BLOCK), :] = (_silu(yn) * _silu(gate)).astype(BF16)
        return carry

    lax.fori_loop(0, ts // ROW_BLOCK, conv_rows, 0)

    pool_win = ROW_BLOCK + POOL_HALO

    def pool_rows(rb, carry):
        base = pl.multiple_of(rb * ROW_BLOCK, ROW_BLOCK)
        pos = (t * ts + base + 1
               + lax.broadcasted_iota(jnp.int32, (ROW_BLOCK, POOL_GROUP_DIM), 0))
        for g, w in enumerate(POOL_WINDOWS):
            sl = slice(g * POOL_GROUP_DIM, (g + 1) * POOL_GROUP_DIM)
            win = pbuf[pl.ds(base, pool_win), sl]
            sm = win
            span = 1
            while span < w:
                sm = sm + pltpu.roll(sm, span, axis=0)
                span *= 2
            tok = win[POOL_HALO:, :]
            cnt = jnp.minimum(pos, w).astype(F32)
            pooled[pl.ds(base, ROW_BLOCK), sl] = (sm[POOL_HALO:, :] / cnt - tok).astype(BF16)
        return carry

    lax.fori_loop(0, ts // ROW_BLOCK, pool_rows, 0)

    for g in range(len(POOL_WINDOWS)):
        sl = slice(g * POOL_GROUP_DIM, (g + 1) * POOL_GROUP_DIM)
        mixed = jnp.dot(pooled[:, sl], pw_ref[g], preferred_element_type=F32) + pb_ref[g:g + 1, :]
        gate = zp_ref[:, wc + g * POOL_GROUP_DIM:wc + (g + 1) * POOL_GROUP_DIM].astype(F32)
        act_p[:, sl] = (mixed * ps_ref[:, sl] * _silu(gate)).astype(BF16)

    lane = lax.broadcasted_iota(jnp.int32, (HALF, LANES), 1)
    col = lax.broadcasted_iota(jnp.int32, (HEADS_PER_PAIR * HALF, HALF_KEYS), 1)
    first_valid = jnp.where(t == 0, KEY_PAD, 0)

    n_blocks = 2 * n_pairs

    def block_rows(i):
        return i // 2, pl.multiple_of((i % 2) * HALF, HALF)

    def scores(i, s_ref):
        hp, row0 = block_rows(i)
        q = qbuf[hp, pl.ds(row0, HALF), :]
        kk = kbuf[hp, pl.ds(row0, HALF_KEYS), :]
        zero = jnp.zeros_like(q)
        q2 = jnp.concatenate([jnp.where(lane < HEAD_DIM, q, zero),
                              jnp.where(lane >= HEAD_DIM, q, zero)], axis=0)
        s = lax.dot_general(q2, kk, (((1,), (1,)), ((), ())), preferred_element_type=F32)
        s = s + bias_ref[hp]
        s_ref[...] = jnp.where(col >= first_valid - row0, s, NEG_INF)

    def softmax_terms(s_ref, p_ref, l_ref):
        s = s_ref[...]
        m = jnp.max(s, axis=-1, keepdims=True)
        p = jnp.exp(s - m)
        p_ref[...] = p.astype(BF16)
        l_ref[...] = jnp.sum(p, axis=-1, keepdims=True)

    def weighted_values(i, p_ref, l_ref):
        hp, row0 = block_rows(i)
        vv = vbuf[hp, pl.ds(row0, HALF_KEYS), :]
        o = jnp.dot(p_ref[...], vv, preferred_element_type=F32) / l_ref[...]
        obuf[hp, pl.ds(row0, HALF), :] = jnp.where(lane < HEAD_DIM, o[0:HALF], o[HALF:2 * HALF])

    def attn_step(i, parity, with_scores=True, with_softmax=True):
        if with_scores:
            scores(i + 2, sbufs[parity])
        if with_softmax:
            softmax_terms(sbufs[1 - parity], pbufs[1 - parity], lbufs[1 - parity])
        weighted_values(i, pbufs[parity], lbufs[parity])

    def attn_pair(j, carry):
        attn_step(2 * j, 0)
        attn_step(2 * j + 1, 1)
        return carry

    scores(0, sbufs[0])
    softmax_terms(sbufs[0], pbufs[0], lbufs[0])
    scores(1, sbufs[1])
    lax.fori_loop(0, n_blocks // 2 - 1, attn_pair, 0)
    attn_step(n_blocks - 2, 0, with_scores=False)
    attn_step(n_blocks - 1, 1, with_scores=False, with_softmax=False)

    for hp in range(n_pairs):
        ls = slice(hp * LANES, (hp + 1) * LANES)
        gate = za_ref[:, 3 * wc + hp * LANES:3 * wc + (hp + 1) * LANES].astype(F32)
        act_a[:, ls] = (obuf[hp] * _silu(gate)).astype(BF16)

    d = o_ref.shape[1]

    def out_rows(rb, carry):
        base = pl.multiple_of(rb * OUT_ROWS, OUT_ROWS)
        rows = pl.ds(base, OUT_ROWS)
        y_c = jnp.dot(act_c[rows, :], wco_ref[...], preferred_element_type=F32)
        y_a = jnp.dot(act_a[rows, :], wao_ref[...], preferred_element_type=F32)
        y_p = jnp.dot(act_p[rows, :], wpo_ref[...], preferred_element_type=F32)
        merged = (_sigmoid(zg_ref[rows, 0:d].astype(F32)) * y_c
                  + _sigmoid(zg_ref[rows, d:2 * d].astype(F32)) * y_a
                  + _sigmoid(zg_ref[rows, 2 * d:3 * d].astype(F32)) * y_p)
        y = jnp.dot(merged.astype(BF16), wout_ref[...], preferred_element_type=F32)
        ms = jnp.mean(y * y, axis=-1, keepdims=True)
        o_ref[rows, :] = x_ref[rows, :] + y * lax.rsqrt(ms + EPS) * postg_ref[...]
        return carry

    lax.fori_loop(0, ts // OUT_ROWS, out_rows, 0)

    ubuf[0:CONV_HALO, :] = ubuf[ts:ts + CONV_HALO, :]
    pbuf[0:POOL_HALO, :] = pbuf[ts:ts + POOL_HALO, :]
    kbuf[:, 0:ts, :] = kbuf[:, ts:2 * ts, :]
    vbuf[:, 0:ts, :] = vbuf[:, ts:2 * ts, :]


def _mixer(x2, z, p, batch, seq):
    tokens, d = x2.shape
    ts = SEQ_TILE
    nt = seq // ts
    wc = p["dw"].shape[1]
    n_pairs = wc // LANES
    row = lambda b, t: b * nt + t
    const = lambda shape: pl.BlockSpec(shape, lambda b, t: (0,) * len(shape),
                                       pipeline_mode=pl.Buffered(1))
    in_specs = [
        pl.BlockSpec((ts, d), lambda b, t: (row(b, t), 0)),
        pl.BlockSpec((ts, 4 * wc), lambda b, t: (row(b, t), 0)),
        pl.BlockSpec((ts, 2 * wc), lambda b, t: (row(b, t), 2)),
        pl.BlockSpec((ts, N_BRANCH * d), lambda b, t: (row(b, t), 1)),
        pl.BlockSpec((ts, 3 * wc), lambda b, t: (row(b, t), 4)),
        const(p["dw"].shape), const((1, wc)), const((1, wc)), const((1, wc)), const((wc, d)),
        const(p["bias"].shape), const((wc, d)),
        const(p["pw"].shape), const(p["pb"].shape), const((1, wc)), const((wc, d)),
        const((d, d)), const((1, d)),
    ]
    return pl.pallas_call(
        _mixer_kernel,
        grid=(batch, nt),
        in_specs=in_specs,
        out_specs=pl.BlockSpec((ts, d), lambda b, t: (row(b, t), 0)),
        out_shape=jax.ShapeDtypeStruct((tokens, d), F32),
        scratch_shapes=[
            pltpu.VMEM((CONV_HALO + ts, wc), F32),
            pltpu.VMEM((POOL_HALO + ts, wc), F32),
            pltpu.VMEM((n_pairs, ts, LANES), BF16),
            pltpu.VMEM((n_pairs, 2 * ts, LANES), BF16),
            pltpu.VMEM((n_pairs, 2 * ts, LANES), BF16),
            pltpu.VMEM((n_pairs, ts, LANES), F32),
            pltpu.VMEM((ts, wc), BF16),
            pltpu.VMEM((ts, wc), BF16),
            pltpu.VMEM((ts, wc), BF16),
            pltpu.VMEM((ts, wc), BF16),
            pltpu.VMEM((HEADS_PER_PAIR * HALF, HALF_KEYS), F32),
            pltpu.VMEM((HEADS_PER_PAIR * HALF, HALF_KEYS), F32),
            pltpu.VMEM((HEADS_PER_PAIR * HALF, HALF_KEYS), BF16),
            pltpu.VMEM((HEADS_PER_PAIR * HALF, HALF_KEYS), BF16),
            pltpu.VMEM((HEADS_PER_PAIR * HALF, 1), F32),
            pltpu.VMEM((HEADS_PER_PAIR * HALF, 1), F32),
        ],
        compiler_params=pltpu.CompilerParams(
            dimension_semantics=("arbitrary", "arbitrary"),
            vmem_limit_bytes=VMEM_LIMIT_BYTES),
        name="mixer",
    )(x2, z, z, z, z,
      p["dw"], p["dwb"], p["lng"], p["lnb"], p["wco"],
      p["bias"], p["wao"],
      p["pw"], p["pb"], p["ps"], p["wpo"],
      p["wout"], p["postg"])


def _band_mask():
    i = np.arange(HALF)[:, None]
    j = np.arange(HALF_KEYS)[None, :]
    start = (i // CHUNK) * CHUNK
    return (j >= start) & (j < start + BAND)


def _attn_bias_table(rel_table):
    n_heads = rel_table.shape[0]
    n_diag = HALF + HALF_KEYS - 1
    rel = np.clip(np.arange(n_diag) - (HALF_KEYS - 1) + KEY_PAD, -MAX_REL, MAX_REL) + MAX_REL
    lo, hi = int(rel[0]), int(rel.max())
    n_ramp = hi - lo
    assert np.array_equal(rel[:n_ramp], np.arange(lo, hi)) and np.all(rel[n_ramp:] == hi)
    w = jnp.concatenate(
        [rel_table[:, lo:hi],
         jnp.broadcast_to(rel_table[:, hi:hi + 1], (n_heads, n_diag - n_ramp))], axis=1)
    wp = jnp.concatenate([w[:, HALF_KEYS - 1::-1], w[:, :HALF_KEYS - 1:-1]], axis=1)
    flat = jnp.broadcast_to(wp[:, None, :], (n_heads, HALF + 1, n_diag)).reshape(n_heads, -1)
    skew = flat[:, :HALF * (n_diag - 1)].reshape(n_heads, HALF, n_diag - 1)
    toeplitz = skew[:, :, :HALF_KEYS]
    table = jnp.where(jnp.asarray(_band_mask())[None], toeplitz, NEG_INF).astype(F32)
    return table.reshape(n_heads // HEADS_PER_PAIR, HEADS_PER_PAIR * HALF, HALF_KEYS)


def _permute_in_cols(w_in, wc, d):
    conv = w_in[:, 0:3 * wc]
    attn = w_in[:, 3 * wc:7 * wc]
    pool = w_in[:, 7 * wc:9 * wc]
    merge = w_in[:, 9 * wc:9 * wc + N_BRANCH * d]
    return jnp.concatenate([attn, pool, merge, conv], axis=1)


def kernel(x, pre_norm_g, post_norm_g, w_in, conv_dw, conv_dw_b, conv_ln_g, conv_ln_b,
           w_conv_out, rel_bias, w_attn_out, pool_w, pool_b, pool_scale, w_pool_out, w_out):
    batch, seq, d = x.shape
    depth = w_in.shape[0]
    wc = conv_dw.shape[2]
    assert seq % SEQ_TILE == 0 and (batch * seq) % INPROJ_TM == 0
    assert w_in.shape[2] % INPROJ_TN == 0
    x2 = x.reshape(batch * seq, d)
    for l in range(depth):
        w_l = _permute_in_cols(w_in[l], wc, d).astype(BF16)
        params = dict(
            dw=jnp.pad(conv_dw[l], ((0, 1), (0, 0))),
            dwb=conv_dw_b[l].reshape(1, wc),
            lng=conv_ln_g[l].reshape(1, wc),
            lnb=conv_ln_b[l].reshape(1, wc),
            wco=w_conv_out[l].astype(BF16),
            bias=_attn_bias_table(rel_bias[l]),
            wao=w_attn_out[l].astype(BF16),
            pw=pool_w[l].astype(BF16),
            pb=pool_b[l],
            ps=pool_scale[l].reshape(1, wc),
            wpo=w_pool_out[l].astype(BF16),
            wout=w_out[l].astype(BF16),
            postg=post_norm_g[l].reshape(1, d),
        )
        z = _inproj(x2, pre_norm_g[l].reshape(1, d), w_l)
        x2 = _mixer(x2, z, params, batch, seq)
    return x2.reshape(batch, seq, d)
```
